```python
import math
import jax, jax.numpy as jnp
from jax import lax
import numpy as np

D_MODEL = 2048
BATCH = 8
SEQ = 4096
DEPTH = 1

GRID_W = 64
CTX_LEN = 256
D_MIX = D_MODEL
GDN_HEADS = 8
GDN_DK = 128
GDN_DV = 128
GDN_W = GDN_HEADS * GDN_DV
FOURIER_W = D_MIX - GDN_W
FOURIER_GROUPS = 8
FOURIER_GC = FOURIER_W // FOURIER_GROUPS
CONV_K = 5
CHUNK = 64
D_FF = 5632
N_MOD = 9
RMS_EPS = 1e-6
L2_EPS = 1e-6
DT_MIN = 1e-3
DT_MAX = 1e-1
P_IN = 4 * GDN_W + 4 * GDN_HEADS + FOURIER_W

kernel_name = 'hybrid_gdn_fnet_macaron_prefix'


def rmsnorm(x, g):
    xf = x.astype(jnp.float32)
    y = xf * lax.rsqrt(jnp.mean(xf * xf, axis=-1, keepdims=True) + RMS_EPS)
    return (y * g.astype(jnp.float32)).astype(x.dtype)


def l2norm(x):
    return x * lax.rsqrt(jnp.sum(x * x, axis=-1, keepdims=True) + L2_EPS)


def pre(x, mod, i, g):
    return rmsnorm(x, g) * (1.0 + mod[3 * i + 1]) + mod[3 * i]


def post(x, y, mod, i, g, res_w):
    return x + res_w * mod[3 * i + 2] * rmsnorm(y, g)


def swiglu(h, w_gate, w_up, w_down):
    return (jax.nn.silu(h @ w_gate) * (h @ w_up)) @ w_down


def dwconv(u, w):
    pad = CONV_K // 2
    return lax.conv_general_dilated(
        u, w[:, None, :].astype(u.dtype), window_strides=(1,), padding=[(pad, pad)],
        dimension_numbers=('NWC', 'WIO', 'NWC'), feature_group_count=u.shape[-1])


def conv_rows(u, w):
    b, l, ch = u.shape
    rows = l // GRID_W
    return dwconv(u.reshape(b * rows, GRID_W, ch), w).reshape(b, l, ch)


def mixer_inputs(h, w_in, conv_w, a_log, dt_bias, conv_fn):
    b, l, _ = h.shape
    hh = GDN_HEADS
    p = h @ w_in
    qkv = jax.nn.silu(conv_fn(p[..., :3 * GDN_W], conv_w))
    z = p[..., 3 * GDN_W:4 * GDN_W]
    a = p[..., 4 * GDN_W:4 * GDN_W + 2 * hh].reshape(b, l, 2, hh).astype(jnp.float32)
    bb = p[..., 4 * GDN_W + 2 * hh:4 * GDN_W + 4 * hh].reshape(b, l, 2, hh).astype(jnp.float32)
    f = p[..., 4 * GDN_W + 4 * hh:]
    heads = lambda t: t.reshape(b, l, hh, -1).transpose(0, 2, 1, 3).astype(jnp.float32)
    q = heads(qkv[..., :GDN_W])
    k = heads(qkv[..., GDN_W:2 * GDN_W])
    v = heads(qkv[..., 2 * GDN_W:])
    q = l2norm(q) * (GDN_DK ** -0.5)
    k = l2norm(k)
    g = -jnp.exp(a_log.astype(jnp.float32)) * jax.nn.softplus(a + dt_bias.astype(jnp.float32))
    g = g.transpose(2, 0, 3, 1)
    beta = jax.nn.sigmoid(bb).transpose(2, 0, 3, 1)
    return q, k, v, beta, g, z, f


def gdn_chunked(q, k, v, beta, g, h0):
    b, hh, l, _ = q.shape
    dv = v.shape[-1]
    n = l // CHUNK
    ch = lambda t: t.reshape(b, hh, n, CHUNK, *t.shape[3:])
    q, k, v, beta, g = ch(q), ch(k), ch(v), ch(beta), ch(g)
    G = jnp.cumsum(g, axis=-1)
    pos = jnp.arange(CHUNK)
    incl = pos[:, None] >= pos[None, :]
    decay = jnp.exp(jnp.where(incl, G[..., :, None] - G[..., None, :], -jnp.inf))
    kk = jnp.einsum('bhnck,bhnmk->bhncm', k, k)
    eye = jnp.eye(CHUNK, dtype=q.dtype)
    a_mat = eye + jnp.where(pos[:, None] > pos[None, :], beta[..., :, None] * decay * kk, 0.0)
    gamma = jnp.exp(G)
    rhs = jnp.concatenate([beta[..., None] * v, (beta * gamma)[..., None] * k], axis=-1)
    sol = lax.linalg.triangular_solve(a_mat, rhs, left_side=True, lower=True, unit_diagonal=True)
    v_t, w = sol[..., :dv], sol[..., dv:]
    p_qk = jnp.einsum('bhnck,bhnmk->bhncm', q, k) * decay
    q_g = q * gamma[..., None]
    k_d = k * jnp.exp(G[..., -1:] - G)[..., None]
    g_last = gamma[..., -1]
    xs = tuple(jnp.moveaxis(t, 2, 0) for t in (q_g, p_qk, v_t, w, k_d, g_last))

    def step(h, xc):
        qg_c, pqk_c, vt_c, w_c, kd_c, gl_c = xc
        u = vt_c - jnp.einsum('bhck,bhkv->bhcv', w_c, h)
        o = jnp.einsum('bhck,bhkv->bhcv', qg_c, h) + jnp.einsum('bhcm,bhmv->bhcv', pqk_c, u)
        h = h * gl_c[..., None, None] + jnp.einsum('bhck,bhcv->bhkv', kd_c, u)
        return h, o

    h_final, o = lax.scan(step, h0, xs)
    return jnp.moveaxis(o, 0, 2).reshape(b, hh, l, dv), h_final


def gdn_bidir(qc, kc, vc, bc, gc, ql, kl, vl, bl, gl):
    b = qc.shape[0]
    outs_c, outs_l = [], []
    for d in range(2):
        flip = (lambda t: jnp.flip(t, axis=2)) if d == 1 else (lambda t: t)
        h0 = jnp.zeros((b, GDN_HEADS, GDN_DK, GDN_DV), jnp.float32)
        o_c, h_c = gdn_chunked(flip(qc), flip(kc), flip(vc), flip(bc[d]), flip(gc[d]), h0)
        o_l, _ = gdn_chunked(flip(ql), flip(kl), flip(vl), flip(bl[d]), flip(gl[d]), h_c)
        outs_c.append(flip(o_c))
        outs_l.append(flip(o_l))
    return outs_c[0] + outs_c[1], outs_l[0] + outs_l[1]


def mixer_output(o, z, f, g_onorm, w_out):
    b, hh, l, dv = o.shape
    o = o.transpose(0, 2, 1, 3)
    zz = z.reshape(b, l, hh, dv).astype(jnp.float32)
    gd = (rmsnorm(o, g_onorm) * jax.nn.silu(zz)).reshape(b, l, GDN_W).astype(z.dtype)
    ff = f.reshape(b, l, FOURIER_GROUPS, FOURIER_GC).astype(jnp.float32)
    fr = jnp.fft.fft2(ff, axes=(1, 3), norm='ortho').real.reshape(b, l, FOURIER_W).astype(z.dtype)
    return jnp.concatenate([gd, fr], axis=-1) @ w_out


def setup_inputs(seed: int = 0) -> dict:
    key = jax.random.key(seed)
    ks = jax.random.split(key, 24)
    f32 = jnp.float32
    nrm = lambda kk, shape, fan: jax.random.normal(kk, shape, f32) * (fan ** -0.5)
    gain = lambda kk, shape: 1.0 + 0.02 * jax.random.normal(kk, shape, f32)
    dt = jnp.exp(jax.random.uniform(ks[14], (DEPTH, 2, GDN_HEADS), f32, math.log(DT_MIN), math.log(DT_MAX)))
    return {
        'x': jax.random.normal(ks[0], (BATCH, SEQ, D_MODEL), f32),
        'c': jax.random.normal(ks[1], (BATCH, D_MODEL), f32),
        'ctx': jax.random.normal(ks[2], (BATCH, CTX_LEN, D_MODEL), f32),
        'c_ctx': jax.random.normal(ks[3], (D_MODEL,), f32),
        'w_mod': 0.5 * nrm(ks[4], (DEPTH, D_MODEL, N_MOD * D_MODEL), D_MODEL),
        'b_mod': 0.02 * jax.random.normal(ks[5], (DEPTH, N_MOD * D_MODEL), f32),
        'g_pre': gain(ks[6], (DEPTH, 3, D_MODEL)),
        'g_post': gain(ks[7], (DEPTH, 3, D_MODEL)),
        'w1_gate': nrm(ks[8], (DEPTH, D_MODEL, D_FF), D_MODEL),
        'w1_up': nrm(ks[9], (DEPTH, D_MODEL, D_FF), D_MODEL),
        'w1_down': nrm(ks[10], (DEPTH, D_FF, D_MODEL), D_FF),
        'w_in': nrm(ks[11], (DEPTH, D_MODEL, P_IN), D_MODEL),
        'conv_w': nrm(ks[12], (DEPTH, CONV_K, 3 * GDN_W), CONV_K),
        'a_log': jnp.log(jax.random.uniform(ks[13], (DEPTH, 2, GDN_HEADS), f32, 1.0, 16.0)),
        'dt_bias': dt + jnp.log(-jnp.expm1(-dt)),
        'g_onorm': gain(ks[15], (DEPTH, GDN_DV)),
        'w_out': nrm(ks[16], (DEPTH, D_MIX, D_MODEL), D_MIX),
        'w2_gate': nrm(ks[17], (DEPTH, D_MODEL, D_FF), D_MODEL),
        'w2_up': nrm(ks[18], (DEPTH, D_MODEL, D_FF), D_MODEL),
        'w2_down': nrm(ks[19], (DEPTH, D_FF, D_MODEL), D_FF),
    }


def reference(x, c, ctx, c_ctx, w_mod, b_mod, g_pre, g_post, w1_gate, w1_up, w1_down,
              w_in, conv_w, a_log, dt_bias, g_onorm, w_out, w2_gate, w2_up, w2_down):
    b = x.shape[0]
    for l in range(DEPTH):
        mod_l = (jax.nn.silu(c) @ w_mod[l] + b_mod[l]).reshape(b, N_MOD, D_MODEL).transpose(1, 0, 2)[:, :, None, :]
        mod_c = (jax.nn.silu(c_ctx) @ w_mod[l] + b_mod[l]).reshape(N_MOD, 1, 1, D_MODEL)
        x = post(x, swiglu(pre(x, mod_l, 0, g_pre[l, 0]), w1_gate[l], w1_up[l], w1_down[l]), mod_l, 0, g_post[l, 0], 0.5)
        ctx = post(ctx, swiglu(pre(ctx, mod_c, 0, g_pre[l, 0]), w1_gate[l], w1_up[l], w1_down[l]), mod_c, 0, g_post[l, 0], 0.5)
        qc, kc, vc, bc, gc, zc, fc = mixer_inputs(pre(ctx, mod_c, 1, g_pre[l, 1]), w_in[l], conv_w[l], a_log[l], dt_bias[l], dwconv)
        ql, kl, vl, bl, gl, zl, fl = mixer_inputs(pre(x, mod_l, 1, g_pre[l, 1]), w_in[l], conv_w[l], a_log[l], dt_bias[l], conv_rows)
        o_c, o_l = gdn_bidir(qc, kc, vc, bc, gc, ql, kl, vl, bl, gl)
        x = post(x, mixer_output(o_l, zl, fl, g_onorm[l], w_out[l]), mod_l, 1, g_post[l, 1], 1.0)
        x = post(x, swiglu(pre(x, mod_l, 2, g_pre[l, 2]), w2_gate[l], w2_up[l], w2_down[l]), mod_l, 2, g_post[l, 2], 0.5)
        if l + 1 < DEPTH:
            ctx = post(ctx, mixer_output(o_c, zc, fc, g_onorm[l], w_out[l]), mod_c, 1, g_post[l, 1], 1.0)
            ctx = post(ctx, swiglu(pre(ctx, mod_c, 2, g_pre[l, 2]), w2_gate[l], w2_up[l], w2_down[l]), mod_c, 2, g_post[l, 2], 0.5)
    return x
```

```python
import functools
import math

import jax
import jax.numpy as jnp
import numpy as np
from jax import lax
from jax.experimental import pallas as pl
from jax.experimental.pallas import tpu as pltpu

F32 = jnp.float32
BF16 = jnp.bfloat16

RMS_EPS = 1e-6
L2_EPS = 1e-6
HEADS = 8
HEAD_DIM = 128
GDN_W = HEADS * HEAD_DIM
CONV_K = 5
CHUNK = 64
SUPER = 256
N_LEVELS = 6
LANES = 128
VMEM_LIMIT = 56 * 1024 * 1024


def _rms(x):
    return x * lax.rsqrt(jnp.mean(x * x, axis=-1, keepdims=True) + RMS_EPS)


def _silu(x):
    return x * jax.nn.sigmoid(x)


def _dot(a, b):
    return jnp.dot(a, b, preferred_element_type=F32)


def _dot_nt(a, b):
    return lax.dot_general(a, b, (((1,), (1,)), ((), ())), preferred_element_type=F32)


def _dot_tn(a, b):
    return lax.dot_general(a, b, (((0,), (0,)), ((), ())), preferred_element_type=F32)


def _dot_exact(a, b):
    return jnp.dot(a, b, preferred_element_type=F32, precision=lax.Precision.HIGHEST)


def _mod_kernel(c_ref, w_ref, b_ref, o_ref):
    s = _silu(c_ref[...]).astype(BF16)
    o_ref[...] = _dot(s, w_ref[...].astype(BF16)) + b_ref[...]


def _mod_call(c_rows, w_mod, b_mod):
    rows, d = c_rows.shape
    n = w_mod.shape[1]
    tn = 1024
    return pl.pallas_call(
        _mod_kernel,
        grid=(n // tn,),
        in_specs=[
            pl.BlockSpec((rows, d), lambda j: (0, 0)),
            pl.BlockSpec((d, tn), lambda j: (0, j)),
            pl.BlockSpec((1, tn), lambda j: (0, j)),
        ],
        out_specs=pl.BlockSpec((rows, tn), lambda j: (0, j)),
        out_shape=jax.ShapeDtypeStruct((rows, n), F32),
        compiler_params=pltpu.CompilerParams(
            dimension_semantics=("arbitrary",), vmem_limit_bytes=VMEM_LIMIT),
        name="mod",
    )(c_rows, w_mod, b_mod.reshape(1, n))


def _ffn_kernel(x_ref, shift_ref, scale_ref, gate_ref, gpre_ref, gpost_ref,
                wg_ref, wu_ref, wd_ref, o_ref, h_scr, acc_scr, *, res_w):
    j = pl.program_id(1)

    @pl.when(j == 0)
    def _():
        x = x_ref[...]
        h = _rms(x) * gpre_ref[...] * (1.0 + scale_ref[0]) + shift_ref[0]
        h_scr[...] = h.astype(BF16)
        acc_scr[...] = jnp.zeros_like(acc_scr)

    h = h_scr[...]
    g = _dot(h, wg_ref[...])
    u = _dot(h, wu_ref[...])
    a = (_silu(g) * u).astype(BF16)
    acc_scr[...] += _dot(a, wd_ref[...])

    @pl.when(j == pl.num_programs(1) - 1)
    def _():
        y = _rms(acc_scr[...]) * gpost_ref[...]
        o_ref[...] = x_ref[...] + (res_w * gate_ref[0]) * y


def _ffn_call(x2d, shift, scale, gate, g_pre, g_post, wg, wu, wd, *, tokens_per_row, res_w):
    n_tok, d = x2d.shape
    f = wg.shape[1]
    tm = min(512, n_tok)
    tf = 512
    assert n_tok % tm == 0 and f % tf == 0
    if tokens_per_row is None:
        row = lambda i, j: (0, 0, 0)
    else:
        assert tokens_per_row % tm == 0
        tiles_per_row = tokens_per_row // tm
        row = lambda i, j: (i // tiles_per_row, 0, 0)
    vec = pl.BlockSpec((1, 1, d), row)
    const = pl.BlockSpec((1, d), lambda i, j: (0, 0))
    return pl.pallas_call(
        functools.partial(_ffn_kernel, res_w=res_w),
        grid=(n_tok // tm, f // tf),
        in_specs=[
            pl.BlockSpec((tm, d), lambda i, j: (i, 0)),
            vec, vec, vec, const, const,
            pl.BlockSpec((d, tf), lambda i, j: (0, j)),
            pl.BlockSpec((d, tf), lambda i, j: (0, j)),
            pl.BlockSpec((tf, d), lambda i, j: (j, 0)),
        ],
        out_specs=pl.BlockSpec((tm, d), lambda i, j: (i, 0)),
        out_shape=jax.ShapeDtypeStruct((n_tok, d), F32),
        scratch_shapes=[pltpu.VMEM((tm, d), BF16), pltpu.VMEM((tm, d), F32)],
        compiler_params=pltpu.CompilerParams(
            dimension_semantics=("arbitrary", "arbitrary"), vmem_limit_bytes=VMEM_LIMIT),
        name="ffn",
    )(x2d, shift, scale, gate, g_pre.reshape(1, d), g_post.reshape(1, d), wg, wu, wd)


def _conv_silu(p, w, pos, row_len):
    n = p.shape[0]
    acc = p * w[CONV_K // 2:CONV_K // 2 + 1, :]
    for tap in range(CONV_K):
        s = tap - CONV_K // 2
        if s == 0:
            continue
        shifted = pltpu.roll(p, (n - s) % n, axis=0)
        ok = (pos + s >= 0) & (pos + s < row_len)
        acc = acc + jnp.where(ok, shifted, 0.0) * w[tap:tap + 1, :]
    return _silu(acc)


def _inproj_kernel(ctx_ref, x_ref, shc_ref, scc_ref, shl_ref, scl_ref, gpre_ref,
                   wqkv_ref, wzf_ref, wab_ref, convw_ref, gatep_ref, cs_ref,
                   q_ref, k_ref, v_ref, z_ref, fa_ref, fb_ref, gb_ref):
    s = pl.program_id(1)

    def body(src_ref, shift, scale, row_len, latent):
        x = src_ref[0]
        h = (_rms(x) * gpre_ref[...] * (1.0 + scale) + shift).astype(BF16)
        pos = lax.broadcasted_iota(jnp.int32, (SUPER, 1), 0) % row_len
        for idx, out_ref in enumerate((q_ref, k_ref, v_ref)):
            cols = slice(idx * GDN_W, (idx + 1) * GDN_W)
            y = _conv_silu(_dot(h, wqkv_ref[:, cols]), convw_ref[:, cols], pos, row_len)
            for hh in range(HEADS):
                yh = y[:, hh * HEAD_DIM:(hh + 1) * HEAD_DIM]
                if idx < 2:
                    yh = yh * lax.rsqrt(jnp.sum(yh * yh, axis=-1, keepdims=True) + L2_EPS)
                if idx == 0:
                    yh = yh * (HEAD_DIM ** -0.5)
                out_ref[0, hh] = yh
        ab = _dot(h, wab_ref[...])
        a_log = gatep_ref[0:1, :]
        dt_bias = gatep_ref[1:2, :]
        xa = ab + dt_bias
        softplus = jnp.maximum(xa, 0.0) + jnp.log1p(jnp.exp(-jnp.abs(xa)))
        g = -jnp.exp(a_log) * softplus
        lane = lax.broadcasted_iota(jnp.int32, (1, LANES), 1)
        gb_ref[0] = jnp.where(lane < 2 * HEADS, g, jax.nn.sigmoid(ab))
        if latent:
            z_ref[0] = _dot(h, wzf_ref[:, :GDN_W])
            f = _dot(h, wzf_ref[:, GDN_W:]).astype(BF16)
            for gg in range(HEADS):
                ab_f = _dot(f[:, gg * HEAD_DIM:(gg + 1) * HEAD_DIM], cs_ref[...])
                fa_ref[0, :, gg * HEAD_DIM:(gg + 1) * HEAD_DIM] = ab_f[:, :HEAD_DIM].astype(BF16)
                fb_ref[0, :, gg * HEAD_DIM:(gg + 1) * HEAD_DIM] = ab_f[:, HEAD_DIM:].astype(BF16)

    @pl.when(s == 0)
    def _():
        body(ctx_ref, shc_ref[0], scc_ref[0], ctx_ref.shape[1], False)

    @pl.when(s > 0)
    def _():
        body(x_ref, shl_ref[0], scl_ref[0], CHUNK, True)


def _inproj_call(ctx1, x1, sh_c, sc_c, sh_l, sc_l, g_pre, wqkv, wzf, wab, conv_w, gatep, cs):
    b, seq, d = x1.shape
    assert ctx1.shape[1] == SUPER and seq % SUPER == 0
    ns = 1 + seq // SUPER
    ltot = SUPER + seq
    lat = lambda bi, s: (bi, jnp.maximum(s - 1, 0), 0)
    resident = lambda shape: pl.BlockSpec(shape, lambda bi, s: (0, 0), pipeline_mode=pl.Buffered(1))
    head_spec = pl.BlockSpec((1, HEADS, SUPER, HEAD_DIM), lambda bi, s: (bi, 0, s, 0))
    lat_spec = pl.BlockSpec((1, SUPER, GDN_W), lat)
    return pl.pallas_call(
        _inproj_kernel,
        grid=(b, ns),
        in_specs=[
            pl.BlockSpec((1, SUPER, d), lambda bi, s: (bi, 0, 0)),
            pl.BlockSpec((1, SUPER, d), lat),
            pl.BlockSpec((1, 1, d), lambda bi, s: (0, 0, 0)),
            pl.BlockSpec((1, 1, d), lambda bi, s: (0, 0, 0)),
            pl.BlockSpec((1, 1, d), lambda bi, s: (bi, 0, 0)),
            pl.BlockSpec((1, 1, d), lambda bi, s: (bi, 0, 0)),
            pl.BlockSpec((1, d), lambda bi, s: (0, 0)),
            resident(wqkv.shape), resident(wzf.shape), resident(wab.shape),
            pl.BlockSpec(conv_w.shape, lambda bi, s: (0, 0)),
            pl.BlockSpec(gatep.shape, lambda bi, s: (0, 0)),
            pl.BlockSpec(cs.shape, lambda bi, s: (0, 0)),
        ],
        out_specs=[head_spec, head_spec, head_spec, lat_spec, lat_spec, lat_spec,
                   pl.BlockSpec((1, SUPER, LANES), lambda bi, s: (bi, s, 0))],
        out_shape=[
            jax.ShapeDtypeStruct((b, HEADS, ltot, HEAD_DIM), F32),
            jax.ShapeDtypeStruct((b, HEADS, ltot, HEAD_DIM), F32),
            jax.ShapeDtypeStruct((b, HEADS, ltot, HEAD_DIM), F32),
            jax.ShapeDtypeStruct((b, seq, GDN_W), F32),
            jax.ShapeDtypeStruct((b, seq, GDN_W), BF16),
            jax.ShapeDtypeStruct((b, seq, GDN_W), BF16),
            jax.ShapeDtypeStruct((b, ltot, LANES), F32),
        ],
        compiler_params=pltpu.CompilerParams(
            dimension_semantics=("arbitrary", "arbitrary"), vmem_limit_bytes=VMEM_LIMIT),
        name="inproj",
    )(ctx1, x1, sh_c, sc_c, sh_l, sc_l, g_pre.reshape(1, d), wqkv, wzf, wab, conv_w, gatep, cs)


def _gdn_masks():
    i = np.arange(SUPER)[:, None]
    j = np.arange(SUPER)[None, :]
    same = (i // CHUNK) == (j // CHUNK)
    out = np.zeros((2, 2 + N_LEVELS, SUPER, SUPER), np.float32)
    for d in range(2):
        a, c = (i, j) if d == 0 else (j, i)
        incl = same & (a >= c)
        out[d, 0] = np.where(incl, 0.0, -np.inf)
        out[d, 1] = incl
        for lvl in range(N_LEVELS):
            out[d, 2 + lvl] = (((a >> (lvl + 1)) == (c >> (lvl + 1)))
                               & (((a >> lvl) & 1) == 1) & (((c >> lvl) & 1) == 0))
    return out


def _gdn_kernel(qf_ref, kf_ref, vf_ref, gbf_ref, qb_ref, kb_ref, vb_ref, gbb_ref, mask_ref,
                of_ref, ob_ref, h_scr):
    s = pl.program_id(1)

    @pl.when(s == 0)
    def _():
        h_scr[...] = jnp.zeros_like(h_scr)

    n_chunks = SUPER // CHUNK
    rows = lax.broadcasted_iota(jnp.int32, (SUPER, SUPER), 0)
    cols = lax.broadcasted_iota(jnp.int32, (SUPER, SUPER), 1)
    eye = (rows == cols).astype(F32)

    for d, (q_ref, k_ref, v_ref, gb_ref, o_ref) in enumerate(
            ((qf_ref, kf_ref, vf_ref, gbf_ref, of_ref), (qb_ref, kb_ref, vb_ref, gbb_ref, ob_ref))):
        gb = gb_ref[0]
        incl = mask_ref[d, 1]
        g_cum = _dot_exact(incl, gb)
        g_cum_t = g_cum.T
        g_tot = _dot_exact(incl + incl.T - eye, gb)
        for hh in range(HEADS):
            col = d * HEADS + hh
            q = q_ref[0, hh]
            k = k_ref[0, hh]
            v = v_ref[0, hh]
            q16 = q.astype(BF16)
            k16 = k.astype(BF16)
            beta = gb[:, 2 * HEADS + col:2 * HEADS + col + 1]
            gc = g_cum[:, col:col + 1]
            gr = g_cum_t[col:col + 1, :]
            decay = jnp.exp((gc - gr) + mask_ref[d, 0])
            n_raw = (beta * decay) * _dot_nt(k16, k16)
            t_inv = eye - n_raw * mask_ref[d, 2]
            for lvl in range(1, N_LEVELS):
                nm = (n_raw * mask_ref[d, 2 + lvl]).astype(BF16)
                t16 = t_inv.astype(BF16)
                t_inv = t_inv - _dot(t16, _dot(nm, t16).astype(BF16))
            gamma = jnp.exp(gc)
            rhs = jnp.concatenate([beta * v, (beta * gamma) * k], axis=1).astype(BF16)
            sol = _dot(t_inv.astype(BF16), rhs)
            v_t = sol[:, :HEAD_DIM]
            w16 = sol[:, HEAD_DIM:].astype(BF16)
            p_qk = (_dot_nt(q16, k16) * decay).astype(BF16)
            q_g = (q * gamma).astype(BF16)
            gt = g_tot[:, col:col + 1]
            k_d = (k * jnp.exp(gt - gc)).astype(BF16)
            h = h_scr[d, hh]
            order = range(n_chunks) if d == 0 else range(n_chunks - 1, -1, -1)
            for c in order:
                r = slice(c * CHUNK, (c + 1) * CHUNK)
                h16 = h.astype(BF16)
                u = v_t[r] - _dot(w16[r], h16)
                u16 = u.astype(BF16)
                o_ref[0, hh, r, :] = _dot(q_g[r], h16) + _dot(p_qk[r, r], u16)
                h = h * jnp.exp(gt[c * CHUNK:c * CHUNK + 1, :]) + _dot_tn(k_d[r], u16)
            h_scr[d, hh] = h


def _gdn_call(q, k, v, gb, masks, seq):
    b = q.shape[0]
    n_lat = seq // SUPER
    ns = 1 + n_lat
    fwd = lambda bi, s: (bi, 0, s, 0)
    bwd = lambda bi, s: (bi, 0, jnp.where(s == 0, 0, ns - s), 0)
    fwd_g = lambda bi, s: (bi, s, 0)
    bwd_g = lambda bi, s: (bi, jnp.where(s == 0, 0, ns - s), 0)
    head = lambda im: pl.BlockSpec((1, HEADS, SUPER, HEAD_DIM), im)
    out_f = lambda bi, s: (bi, 0, jnp.maximum(s - 1, 0), 0)
    out_b = lambda bi, s: (bi, 0, jnp.where(s == 0, n_lat - 1, n_lat - s), 0)
    o_shape = jax.ShapeDtypeStruct((b, HEADS, seq, HEAD_DIM), F32)
    return pl.pallas_call(
        _gdn_kernel,
        grid=(b, ns),
        in_specs=[head(fwd), head(fwd), head(fwd), pl.BlockSpec((1, SUPER, LANES), fwd_g),
                  head(bwd), head(bwd), head(bwd), pl.BlockSpec((1, SUPER, LANES), bwd_g),
                  pl.BlockSpec(masks.shape, lambda bi, s: (0, 0, 0, 0))],
        out_specs=[head(out_f), head(out_b)],
        out_shape=[o_shape, o_shape],
        scratch_shapes=[pltpu.VMEM((2, HEADS, HEAD_DIM, HEAD_DIM), F32)],
        compiler_params=pltpu.CompilerParams(
            dimension_semantics=("arbitrary", "arbitrary"), vmem_limit_bytes=VMEM_LIMIT),
        name="gdn",
    )(q, k, v, gb, q, k, v, gb, masks)


def _dft_kernel(wc_ref, ws_ref, fa_ref, fb_ref, o_ref, acc_scr):
    kk = pl.program_id(2)

    @pl.when(kk == 0)
    def _():
        acc_scr[...] = jnp.zeros_like(acc_scr)

    acc_scr[...] += _dot(wc_ref[...], fa_ref[0]) + _dot(ws_ref[...], fb_ref[0])

    @pl.when(kk == pl.num_programs(2) - 1)
    def _():
        o_ref[0] = acc_scr[...]


def _dft_call(wc, ws_neg, fa, fb):
    b, seq, n = fa.shape
    tm = min(1024, seq)
    tk = min(1024, seq)
    return pl.pallas_call(
        _dft_kernel,
        grid=(b, seq // tm, seq // tk),
        in_specs=[
            pl.BlockSpec((tm, tk), lambda bi, m, kk: (m, kk)),
            pl.BlockSpec((tm, tk), lambda bi, m, kk: (m, kk)),
            pl.BlockSpec((1, tk, n), lambda bi, m, kk: (bi, kk, 0)),
            pl.BlockSpec((1, tk, n), lambda bi, m, kk: (bi, kk, 0)),
        ],
        out_specs=pl.BlockSpec((1, tm, n), lambda bi, m, kk: (bi, m, 0)),
        out_shape=jax.ShapeDtypeStruct((b, seq, n), F32),
        scratch_shapes=[pltpu.VMEM((tm, n), F32)],
        compiler_params=pltpu.CompilerParams(
            dimension_semantics=("arbitrary", "arbitrary", "arbitrary"), vmem_limit_bytes=VMEM_LIMIT),
        name="dft",
    )(wc, ws_neg, fa, fb)


def _outproj_kernel(of_ref, ob_ref, z_ref, fr_ref, x_ref, gate_ref, gon_ref, gpost_ref, w_ref, o_ref):
    z = z_ref[0]
    parts = []
    for hh in range(HEADS):
        o = of_ref[0, hh] + ob_ref[0, hh]
        zh = z[:, hh * HEAD_DIM:(hh + 1) * HEAD_DIM]
        parts.append((_rms(o) * gon_ref[...] * _silu(zh)).astype(BF16))
    parts.append(fr_ref[0].astype(BF16))
    cat = jnp.concatenate(parts, axis=1)
    y = _rms(_dot(cat, w_ref[...])) * gpost_ref[...]
    o_ref[0] = x_ref[0] + gate_ref[0] * y


def _outproj_call(o_f, o_b, z, fr, x1, gate, g_onorm, g_post, w_out):
    b, seq, d = x1.shape
    tm = min(256, seq)
    tok = lambda n: pl.BlockSpec((1, tm, n), lambda bi, i: (bi, i, 0))
    head = pl.BlockSpec((1, HEADS, tm, HEAD_DIM), lambda bi, i: (bi, 0, i, 0))
    return pl.pallas_call(
        _outproj_kernel,
        grid=(b, seq // tm),
        in_specs=[head, head, tok(GDN_W), tok(GDN_W), tok(d),
                  pl.BlockSpec((1, 1, d), lambda bi, i: (bi, 0, 0)),
                  pl.BlockSpec((1, HEAD_DIM), lambda bi, i: (0, 0)),
                  pl.BlockSpec((1, d), lambda bi, i: (0, 0)),
                  pl.BlockSpec(w_out.shape, lambda bi, i: (0, 0), pipeline_mode=pl.Buffered(1))],
        out_specs=tok(d),
        out_shape=jax.ShapeDtypeStruct((b, seq, d), F32),
        compiler_params=pltpu.CompilerParams(
            dimension_semantics=("arbitrary", "arbitrary"), vmem_limit_bytes=VMEM_LIMIT),
        name="outproj",
    )(o_f, o_b, z, fr, x1, gate, g_onorm.reshape(1, HEAD_DIM), g_post.reshape(1, d), w_out)


def _dft_tables(seq):
    p = lax.broadcasted_iota(jnp.int32, (seq, seq), 0)
    l = lax.broadcasted_iota(jnp.int32, (seq, seq), 1)
    ang = ((p * l) % seq).astype(F32) * (2.0 * math.pi / seq)
    return jnp.cos(ang).astype(BF16), (-jnp.sin(ang)).astype(BF16)


def _channel_table(seq):
    c = np.arange(HEAD_DIM)
    ang = 2.0 * np.pi * ((c[:, None] * c[None, :]) % HEAD_DIM) / HEAD_DIM
    scale = 1.0 / math.sqrt(seq * HEAD_DIM)
    return np.concatenate([np.cos(ang), np.sin(ang)], axis=1).astype(np.float32) * scale


def kernel(x, c, ctx, c_ctx, w_mod, b_mod, g_pre, g_post, w1_gate, w1_up, w1_down, w_in, conv_w,
           a_log, dt_bias, g_onorm, w_out, w2_gate, w2_up, w2_down):
    b, seq, d = x.shape
    ctx_len = ctx.shape[1]
    depth = w_mod.shape[0]
    assert depth == 1 and ctx_len == SUPER
    n_mod = w_mod.shape[2] // d
    lyr = 0

    mod_rows = 16
    c_rows = jnp.zeros((mod_rows, d), F32).at[:b].set(c).at[b].set(c_ctx)
    mod = _mod_call(c_rows, w_mod[lyr], b_mod[lyr]).reshape(mod_rows, n_mod, 1, d)
    mod_l = lambda i: mod[:b, i]
    mod_c = lambda i: mod[b:b + 1, i]

    cast = lambda w: w.astype(BF16)
    w1 = (cast(w1_gate[lyr]), cast(w1_up[lyr]), cast(w1_down[lyr]))
    w2 = (cast(w2_gate[lyr]), cast(w2_up[lyr]), cast(w2_down[lyr]))

    x1 = _ffn_call(x.reshape(b * seq, d), mod_l(0), mod_l(1), mod_l(2), g_pre[lyr, 0], g_post[lyr, 0],
                   *w1, tokens_per_row=seq, res_w=0.5).reshape(b, seq, d)
    ctx1 = _ffn_call(ctx.reshape(b * ctx_len, d), mod_c(0), mod_c(1), mod_c(2), g_pre[lyr, 0],
                     g_post[lyr, 0], *w1, tokens_per_row=None, res_w=0.5).reshape(b, ctx_len, d)

    wi = w_in[lyr]
    n_gate = 4 * HEADS
    wqkv = cast(wi[:, :3 * GDN_W])
    wzf = cast(jnp.concatenate([wi[:, 3 * GDN_W:4 * GDN_W], wi[:, 4 * GDN_W + n_gate:]], axis=1))
    wab = cast(jnp.pad(wi[:, 4 * GDN_W:4 * GDN_W + n_gate], ((0, 0), (0, LANES - n_gate))))
    gatep = jnp.zeros((2, LANES), F32)
    gatep = gatep.at[0, :2 * HEADS].set(a_log[lyr].reshape(-1)).at[1, :2 * HEADS].set(dt_bias[lyr].reshape(-1))
    cs = jnp.asarray(_channel_table(seq), BF16)
    q, k, v, z, fa, fb, gb = _inproj_call(ctx1, x1, mod_c(3), mod_c(4), mod_l(3), mod_l(4),
                                          g_pre[lyr, 1], wqkv, wzf, wab, conv_w[lyr], gatep, cs)

    o_f, o_b = _gdn_call(q, k, v, gb, jnp.asarray(_gdn_masks()), seq)
    wc, ws_neg = _dft_tables(seq)
    fr = _dft_call(wc, ws_neg, fa, fb)
    x2 = _outproj_call(o_f, o_b, z, fr, x1, mod_l(5), g_onorm[lyr], g_post[lyr, 1], cast(w_out[lyr]))

    out = _ffn_call(x2.reshape(b * seq, d), mod_l(6), mod_l(7), mod_l(8), g_pre[lyr, 2], g_post[lyr, 2],
                    *w2, tokens_per_row=seq, res_w=0.5)
    return out.reshape(b, seq, d)
```

```python
import functools
import math

import jax
import jax.numpy as jnp
import numpy as np
from jax import lax
from jax.experimental import pallas as pl
from jax.experimental.pallas import tpu as pltpu

F32 = jnp.float32
BF16 = jnp.bfloat16

RMS_EPS = 1e-6
L2_EPS = 1e-6
HEADS = 8
HEAD_DIM = 128
GDN_W = HEADS * HEAD_DIM
CONV_K = 5
CHUNK = 64
SUPER = 256
N_LEVELS = 6
LANES = 128
VMEM_LIMIT = 56 * 1024 * 1024


def _rms(x):
    return x * lax.rsqrt(jnp.mean(x * x, axis=-1, keepdims=True) + RMS_EPS)


def _silu(x):
    return x * jax.nn.sigmoid(x)


def _dot(a, b):
    return jnp.dot(a, b, preferred_element_type=F32)


def _dot_nt(a, b):
    return lax.dot_general(a, b, (((1,), (1,)), ((), ())), preferred_element_type=F32)


def _dot_tn(a, b):
    return lax.dot_general(a, b, (((0,), (0,)), ((), ())), preferred_element_type=F32)


def _dot_exact(a, b):
    return jnp.dot(a, b, preferred_element_type=F32, precision=lax.Precision.HIGHEST)


def _mod_kernel(c_ref, w_ref, b_ref, o_ref):
    s = _silu(c_ref[...]).astype(BF16)
    o_ref[...] = _dot(s, w_ref[...].astype(BF16)) + b_ref[...]


def _mod_call(c_rows, w_mod, b_mod):
    rows, d = c_rows.shape
    n = w_mod.shape[1]
    tn = 1024
    return pl.pallas_call(
        _mod_kernel,
        grid=(n // tn,),
        in_specs=[
            pl.BlockSpec((rows, d), lambda j: (0, 0)),
            pl.BlockSpec((d, tn), lambda j: (0, j)),
            pl.BlockSpec((1, tn), lambda j: (0, j)),
        ],
        out_specs=pl.BlockSpec((rows, tn), lambda j: (0, j)),
        out_shape=jax.ShapeDtypeStruct((rows, n), F32),
        compiler_params=pltpu.CompilerParams(
            dimension_semantics=("arbitrary",), vmem_limit_bytes=VMEM_LIMIT),
        name="mod",
    )(c_rows, w_mod, b_mod.reshape(1, n))


def _ffn_kernel(x_ref, shift_ref, scale_ref, gate_ref, gpre_ref, gpost_ref,
                wg_ref, wu_ref, wd_ref, o_ref, h_scr, acc_scr, *, res_w):
    j = pl.program_id(1)

    @pl.when(j == 0)
    def _():
        x = x_ref[...]
        h = _rms(x) * gpre_ref[...] * (1.0 + scale_ref[0]) + shift_ref[0]
        h_scr[...] = h.astype(BF16)
        acc_scr[...] = jnp.zeros_like(acc_scr)

    h = h_scr[...]
    g = _dot(h, wg_ref[...])
    u = _dot(h, wu_ref[...])
    a = (_silu(g) * u).astype(BF16)
    acc_scr[...] += _dot(a, wd_ref[...])

    @pl.when(j == pl.num_programs(1) - 1)
    def _():
        y = _rms(acc_scr[...]) * gpost_ref[...]
        o_ref[...] = x_ref[...] + (res_w * gate_ref[0]) * y


def _ffn_call(x2d, shift, scale, gate, g_pre, g_post, wg, wu, wd, *, tokens_per_row, res_w):
    n_tok, d = x2d.shape
    f = wg.shape[1]
    tm = min(512, n_tok)
    tf = 512
    assert n_tok % tm == 0 and f % tf == 0
    if tokens_per_row is None:
        row = lambda i, j: (0, 0, 0)
    else:
        assert tokens_per_row % tm == 0
        tiles_per_row = tokens_per_row // tm
        row = lambda i, j: (i // tiles_per_row, 0, 0)
    vec = pl.BlockSpec((1, 1, d), row)
    const = pl.BlockSpec((1, d), lambda i, j: (0, 0))
    return pl.pallas_call(
        functools.partial(_ffn_kernel, res_w=res_w),
        grid=(n_tok // tm, f // tf),
        in_specs=[
            pl.BlockSpec((tm, d), lambda i, j: (i, 0)),
            vec, vec, vec, const, const,
            pl.BlockSpec((d, tf), lambda i, j: (0, j)),
            pl.BlockSpec((d, tf), lambda i, j: (0, j)),
            pl.BlockSpec((tf, d), lambda i, j: (j, 0)),
        ],
        out_specs=pl.BlockSpec((tm, d), lambda i, j: (i, 0)),
        out_shape=jax.ShapeDtypeStruct((n_tok, d), F32),
        scratch_shapes=[pltpu.VMEM((tm, d), BF16), pltpu.VMEM((tm, d), F32)],
        compiler_params=pltpu.CompilerParams(
            dimension_semantics=("arbitrary", "arbitrary"), vmem_limit_bytes=VMEM_LIMIT),
        name="ffn",
    )(x2d, shift, scale, gate, g_pre.reshape(1, d), g_post.reshape(1, d), wg, wu, wd)


def _conv_silu(p, w, pos, row_len):
    n = p.shape[0]
    acc = p * w[CONV_K // 2:CONV_K // 2 + 1, :]
    for tap in range(CONV_K):
        s = tap - CONV_K // 2
        if s == 0:
            continue
        shifted = pltpu.roll(p, (n - s) % n, axis=0)
        ok = (pos + s >= 0) & (pos + s < row_len)
        acc = acc + jnp.where(ok, shifted, 0.0) * w[tap:tap + 1, :]
    return _silu(acc)


def _inproj_kernel(ctx_ref, x_ref, shc_ref, scc_ref, shl_ref, scl_ref, gpre_ref,
                   wqkv_ref, wzf_ref, wab_ref, convw_ref, gatep_ref, cs_ref,
                   q_ref, k_ref, v_ref, z_ref, fa_ref, fb_ref, gb_ref):
    s = pl.program_id(1)

    def body(src_ref, shift, scale, row_len, latent):
        x = src_ref[0]
        h = (_rms(x) * gpre_ref[...] * (1.0 + scale) + shift).astype(BF16)
        pos = lax.broadcasted_iota(jnp.int32, (SUPER, 1), 0) % row_len
        for idx, out_ref in enumerate((q_ref, k_ref, v_ref)):
            cols = slice(idx * GDN_W, (idx + 1) * GDN_W)
            y = _conv_silu(_dot(h, wqkv_ref[:, cols]), convw_ref[:, cols], pos, row_len)
            for hh in range(HEADS):
                yh = y[:, hh * HEAD_DIM:(hh + 1) * HEAD_DIM]
                if idx < 2:
                    yh = yh * lax.rsqrt(jnp.sum(yh * yh, axis=-1, keepdims=True) + L2_EPS)
                if idx == 0:
                    yh = yh * (HEAD_DIM ** -0.5)
                out_ref[0, hh] = yh
        ab = _dot(h, wab_ref[...])
        a_log = gatep_ref[0:1, :]
        dt_bias = gatep_ref[1:2, :]
        xa = ab + dt_bias
        softplus = jnp.maximum(xa, 0.0) + jnp.log1p(jnp.exp(-jnp.abs(xa)))
        g = -jnp.exp(a_log) * softplus
        lane = lax.broadcasted_iota(jnp.int32, (1, LANES), 1)
        gb_ref[0] = jnp.where(lane < 2 * HEADS, g, jax.nn.sigmoid(ab))
        if latent:
            z_ref[0] = _dot(h, wzf_ref[:, :GDN_W])
            f = _dot(h, wzf_ref[:, GDN_W:]).astype(BF16)
            for gg in range(HEADS):
                ab_f = _dot(f[:, gg * HEAD_DIM:(gg + 1) * HEAD_DIM], cs_ref[...])
                fa_ref[0, :, gg * HEAD_DIM:(gg + 1) * HEAD_DIM] = ab_f[:, :HEAD_DIM].astype(BF16)
                fb_ref[0, :, gg * HEAD_DIM:(gg + 1) * HEAD_DIM] = ab_f[:, HEAD_DIM:].astype(BF16)

    @pl.when(s == 0)
    def _():
        body(ctx_ref, shc_ref[0], scc_ref[0], ctx_ref.shape[1], False)

    @pl.when(s > 0)
    def _():
        body(x_ref, shl_ref[0], scl_ref[0], CHUNK, True)


def _inproj_call(ctx1, x1, sh_c, sc_c, sh_l, sc_l, g_pre, wqkv, wzf, wab, conv_w, gatep, cs):
    b, seq, d = x1.shape
    assert ctx1.shape[1] == SUPER and seq % SUPER == 0
    ns = 1 + seq // SUPER
    ltot = SUPER + seq
    lat = lambda bi, s: (bi, jnp.maximum(s - 1, 0), 0)
    resident = lambda shape: pl.BlockSpec(shape, lambda bi, s: (0, 0), pipeline_mode=pl.Buffered(1))
    head_spec = pl.BlockSpec((1, HEADS, SUPER, HEAD_DIM), lambda bi, s: (bi, 0, s, 0))
    lat_spec = pl.BlockSpec((1, SUPER, GDN_W), lat)
    return pl.pallas_call(
        _inproj_kernel,
        grid=(b, ns),
        in_specs=[
            pl.BlockSpec((1, SUPER, d), lambda bi, s: (bi, 0, 0)),
            pl.BlockSpec((1, SUPER, d), lat),
            pl.BlockSpec((1, 1, d), lambda bi, s: (0, 0, 0)),
            pl.BlockSpec((1, 1, d), lambda bi, s: (0, 0, 0)),
            pl.BlockSpec((1, 1, d), lambda bi, s: (bi, 0, 0)),
            pl.BlockSpec((1, 1, d), lambda bi, s: (bi, 0, 0)),
            pl.BlockSpec((1, d), lambda bi, s: (0, 0)),
            resident(wqkv.shape), resident(wzf.shape), resident(wab.shape),
            pl.BlockSpec(conv_w.shape, lambda bi, s: (0, 0)),
            pl.BlockSpec(gatep.shape, lambda bi, s: (0, 0)),
            pl.BlockSpec(cs.shape, lambda bi, s: (0, 0)),
        ],
        out_specs=[head_spec, head_spec, head_spec, lat_spec, lat_spec, lat_spec,
                   pl.BlockSpec((1, SUPER, LANES), lambda bi, s: (bi, s, 0))],
        out_shape=[
            jax.ShapeDtypeStruct((b, HEADS, ltot, HEAD_DIM), F32),
            jax.ShapeDtypeStruct((b, HEADS, ltot, HEAD_DIM), F32),
            jax.ShapeDtypeStruct((b, HEADS, ltot, HEAD_DIM), F32),
            jax.ShapeDtypeStruct((b, seq, GDN_W), F32),
            jax.ShapeDtypeStruct((b, seq, GDN_W), BF16),
            jax.ShapeDtypeStruct((b, seq, GDN_W), BF16),
            jax.ShapeDtypeStruct((b, ltot, LANES), F32),
        ],
        compiler_params=pltpu.CompilerParams(
            dimension_semantics=("arbitrary", "arbitrary"), vmem_limit_bytes=VMEM_LIMIT),
        name="inproj",
    )(ctx1, x1, sh_c, sc_c, sh_l, sc_l, g_pre.reshape(1, d), wqkv, wzf, wab, conv_w, gatep, cs)


def _gdn_masks():
    i = np.arange(SUPER)[:, None]
    j = np.arange(SUPER)[None, :]
    same = (i // CHUNK) == (j // CHUNK)
    neg = np.zeros((2, SUPER, SUPER), np.float32)
    incl = np.zeros((2, SUPER, SUPER), np.float32)
    lvl_masks = np.zeros((2, N_LEVELS, SUPER, SUPER), np.float32)
    for d in range(2):
        a, c = (i, j) if d == 0 else (j, i)
        causal = same & (a >= c)
        neg[d] = np.where(causal, 0.0, -np.inf)
        incl[d] = causal
        for lvl in range(N_LEVELS):
            lvl_masks[d, lvl] = (((a >> (lvl + 1)) == (c >> (lvl + 1)))
                                 & (((a >> lvl) & 1) == 1) & (((c >> lvl) & 1) == 0))
    return jnp.asarray(neg), jnp.asarray(incl), jnp.asarray(lvl_masks, BF16)


def _gdn_kernel(qf_ref, kf_ref, vf_ref, gbf_ref, qb_ref, kb_ref, vb_ref, gbb_ref,
                neg_ref, incl_ref, lvl_ref, of_ref, ob_ref, h_scr):
    s = pl.program_id(1)

    @pl.when(s == 0)
    def _():
        h_scr[...] = jnp.zeros_like(h_scr)

    n_chunks = SUPER // CHUNK
    rows = lax.broadcasted_iota(jnp.int32, (SUPER, SUPER), 0)
    cols = lax.broadcasted_iota(jnp.int32, (SUPER, SUPER), 1)
    eye = (rows == cols).astype(F32)
    dirs = ((qf_ref, kf_ref, vf_ref, gbf_ref, of_ref), (qb_ref, kb_ref, vb_ref, gbb_ref, ob_ref))
    problems = [(hh, d) for hh in range(HEADS) for d in range(2)]

    gates = []
    for d in range(2):
        gb = dirs[d][3][0]
        incl = incl_ref[d]
        g_cum = _dot_exact(incl, gb)
        g_tot = _dot_exact(incl + incl.T - eye, gb)
        gates.append((gb, g_cum, g_cum.T, g_tot))

    n16, t_inv, pqk = {}, {}, {}
    for hh, d in problems:
        q_ref, k_ref = dirs[d][0], dirs[d][1]
        gb, g_cum, g_cum_t, _ = gates[d]
        col = d * HEADS + hh
        k16 = k_ref[0, hh].astype(BF16)
        beta = gb[:, 2 * HEADS + col:2 * HEADS + col + 1]
        dec = jnp.exp((g_cum[:, col:col + 1] - g_cum_t[col:col + 1, :]) + neg_ref[d])
        n_raw = (beta * dec) * _dot_nt(k16, k16)
        t_inv[hh, d] = eye - n_raw * lvl_ref[d, 0].astype(F32)
        n16[hh, d] = n_raw.astype(BF16)
        pqk[hh, d] = (_dot_nt(q_ref[0, hh].astype(BF16), k16) * dec).astype(BF16)

    for lvl in range(1, N_LEVELS):
        t16 = {p: t_inv[p].astype(BF16) for p in problems}
        inner = {p: _dot(n16[p] * lvl_ref[p[1], lvl], t16[p]).astype(BF16) for p in problems}
        for p in problems:
            t_inv[p] = t_inv[p] - _dot(t16[p], inner[p])

    v_t, w16, q_g, k_d, g_last = {}, {}, {}, {}, {}
    for hh, d in problems:
        q_ref, k_ref, v_ref = dirs[d][0], dirs[d][1], dirs[d][2]
        gb, g_cum, _, g_tot = gates[d]
        col = d * HEADS + hh
        k = k_ref[0, hh]
        beta = gb[:, 2 * HEADS + col:2 * HEADS + col + 1]
        gc = g_cum[:, col:col + 1]
        gt = g_tot[:, col:col + 1]
        gamma = jnp.exp(gc)
        rhs = jnp.concatenate([beta * v_ref[0, hh], (beta * gamma) * k], axis=1).astype(BF16)
        sol = _dot(t_inv[hh, d].astype(BF16), rhs)
        v_t[hh, d] = sol[:, :HEAD_DIM]
        w16[hh, d] = sol[:, HEAD_DIM:].astype(BF16)
        q_g[hh, d] = (q_ref[0, hh] * gamma).astype(BF16)
        k_d[hh, d] = (k * jnp.exp(gt - gc)).astype(BF16)
        g_last[hh, d] = jnp.exp(gt)

    h = {(hh, d): h_scr[d, hh] for hh, d in problems}
    for step in range(n_chunks):
        rs = {p: (step if p[1] == 0 else n_chunks - 1 - step) * CHUNK for p in problems}
        h16 = {p: h[p].astype(BF16) for p in problems}
        u16 = {}
        for p in problems:
            r = slice(rs[p], rs[p] + CHUNK)
            u16[p] = (v_t[p][r] - _dot(w16[p][r], h16[p])).astype(BF16)
        for p in problems:
            hh, d = p
            r = slice(rs[p], rs[p] + CHUNK)
            dirs[d][4][0, hh, r, :] = _dot(q_g[p][r], h16[p]) + _dot(pqk[p][r, r], u16[p])
            h[p] = h[p] * g_last[p][rs[p]:rs[p] + 1, :] + _dot_tn(k_d[p][r], u16[p])
    for hh, d in problems:
        h_scr[d, hh] = h[hh, d]


def _gdn_call(q, k, v, gb, seq):
    b = q.shape[0]
    neg, incl, lvl_masks = _gdn_masks()
    n_lat = seq // SUPER
    ns = 1 + n_lat
    fwd = lambda bi, s: (bi, 0, s, 0)
    bwd = lambda bi, s: (bi, 0, jnp.where(s == 0, 0, ns - s), 0)
    fwd_g = lambda bi, s: (bi, s, 0)
    bwd_g = lambda bi, s: (bi, jnp.where(s == 0, 0, ns - s), 0)
    head = lambda im: pl.BlockSpec((1, HEADS, SUPER, HEAD_DIM), im)
    out_f = lambda bi, s: (bi, 0, jnp.maximum(s - 1, 0), 0)
    out_b = lambda bi, s: (bi, 0, jnp.where(s == 0, n_lat - 1, n_lat - s), 0)
    o_shape = jax.ShapeDtypeStruct((b, HEADS, seq, HEAD_DIM), F32)
    return pl.pallas_call(
        _gdn_kernel,
        grid=(b, ns),
        in_specs=[head(fwd), head(fwd), head(fwd), pl.BlockSpec((1, SUPER, LANES), fwd_g),
                  head(bwd), head(bwd), head(bwd), pl.BlockSpec((1, SUPER, LANES), bwd_g),
                  pl.BlockSpec(neg.shape, lambda bi, s: (0, 0, 0)),
                  pl.BlockSpec(incl.shape, lambda bi, s: (0, 0, 0)),
                  pl.BlockSpec(lvl_masks.shape, lambda bi, s: (0, 0, 0, 0))],
        out_specs=[head(out_f), head(out_b)],
        out_shape=[o_shape, o_shape],
        scratch_shapes=[pltpu.VMEM((2, HEADS, HEAD_DIM, HEAD_DIM), F32)],
        compiler_params=pltpu.CompilerParams(
            dimension_semantics=("arbitrary", "arbitrary"), vmem_limit_bytes=VMEM_LIMIT),
        name="gdn",
    )(q, k, v, gb, q, k, v, gb, neg, incl, lvl_masks)


def _dft_kernel(wc_ref, ws_ref, fa_ref, fb_ref, o_ref, acc_scr):
    kk = pl.program_id(2)

    @pl.when(kk == 0)
    def _():
        acc_scr[...] = jnp.zeros_like(acc_scr)

    acc_scr[...] += _dot(wc_ref[...], fa_ref[0]) + _dot(ws_ref[...], fb_ref[0])

    @pl.when(kk == pl.num_programs(2) - 1)
    def _():
        o_ref[0] = acc_scr[...]


def _dft_call(wc, ws_neg, fa, fb):
    b, seq, n = fa.shape
    tm = min(1024, seq)
    tk = min(1024, seq)
    return pl.pallas_call(
        _dft_kernel,
        grid=(b, seq // tm, seq // tk),
        in_specs=[
            pl.BlockSpec((tm, tk), lambda bi, m, kk: (m, kk)),
            pl.BlockSpec((tm, tk), lambda bi, m, kk: (m, kk)),
            pl.BlockSpec((1, tk, n), lambda bi, m, kk: (bi, kk, 0)),
            pl.BlockSpec((1, tk, n), lambda bi, m, kk: (bi, kk, 0)),
        ],
        out_specs=pl.BlockSpec((1, tm, n), lambda bi, m, kk: (bi, m, 0)),
        out_shape=jax.ShapeDtypeStruct((b, seq, n), F32),
        scratch_shapes=[pltpu.VMEM((tm, n), F32)],
        compiler_params=pltpu.CompilerParams(
            dimension_semantics=("arbitrary", "arbitrary", "arbitrary"), vmem_limit_bytes=VMEM_LIMIT),
        name="dft",
    )(wc, ws_neg, fa, fb)


def _outproj_kernel(of_ref, ob_ref, z_ref, fr_ref, x_ref, gate_ref, gon_ref, gpost_ref, w_ref, o_ref):
    z = z_ref[0]
    parts = []
    for hh in range(HEADS):
        o = of_ref[0, hh] + ob_ref[0, hh]
        zh = z[:, hh * HEAD_DIM:(hh + 1) * HEAD_DIM]
        parts.append((_rms(o) * gon_ref[...] * _silu(zh)).astype(BF16))
    parts.append(fr_ref[0].astype(BF16))
    cat = jnp.concatenate(parts, axis=1)
    y = _rms(_dot(cat, w_ref[...])) * gpost_ref[...]
    o_ref[0] = x_ref[0] + gate_ref[0] * y


def _outproj_call(o_f, o_b, z, fr, x1, gate, g_onorm, g_post, w_out):
    b, seq, d = x1.shape
    tm = min(256, seq)
    tok = lambda n: pl.BlockSpec((1, tm, n), lambda bi, i: (bi, i, 0))
    head = pl.BlockSpec((1, HEADS, tm, HEAD_DIM), lambda bi, i: (bi, 0, i, 0))
    return pl.pallas_call(
        _outproj_kernel,
        grid=(b, seq // tm),
        in_specs=[head, head, tok(GDN_W), tok(GDN_W), tok(d),
                  pl.BlockSpec((1, 1, d), lambda bi, i: (bi, 0, 0)),
                  pl.BlockSpec((1, HEAD_DIM), lambda bi, i: (0, 0)),
                  pl.BlockSpec((1, d), lambda bi, i: (0, 0)),
                  pl.BlockSpec(w_out.shape, lambda bi, i: (0, 0), pipeline_mode=pl.Buffered(1))],
        out_specs=tok(d),
        out_shape=jax.ShapeDtypeStruct((b, seq, d), F32),
        compiler_params=pltpu.CompilerParams(
            dimension_semantics=("arbitrary", "arbitrary"), vmem_limit_bytes=VMEM_LIMIT),
        name="outproj",
    )(o_f, o_b, z, fr, x1, gate, g_onorm.reshape(1, HEAD_DIM), g_post.reshape(1, d), w_out)


def _dft_tables(seq):
    p = lax.broadcasted_iota(jnp.int32, (seq, seq), 0)
    l = lax.broadcasted_iota(jnp.int32, (seq, seq), 1)
    ang = ((p * l) % seq).astype(F32) * (2.0 * math.pi / seq)
    return jnp.cos(ang).astype(BF16), (-jnp.sin(ang)).astype(BF16)


def _channel_table(seq):
    c = np.arange(HEAD_DIM)
    ang = 2.0 * np.pi * ((c[:, None] * c[None, :]) % HEAD_DIM) / HEAD_DIM
    scale = 1.0 / math.sqrt(seq * HEAD_DIM)
    return np.concatenate([np.cos(ang), np.sin(ang)], axis=1).astype(np.float32) * scale


def kernel(x, c, ctx, c_ctx, w_mod, b_mod, g_pre, g_post, w1_gate, w1_up, w1_down, w_in, conv_w,
           a_log, dt_bias, g_onorm, w_out, w2_gate, w2_up, w2_down):
    b, seq, d = x.shape
    ctx_len = ctx.shape[1]
    depth = w_mod.shape[0]
    assert depth == 1 and ctx_len == SUPER
    n_mod = w_mod.shape[2] // d
    lyr = 0

    mod_rows = 16
    c_rows = jnp.zeros((mod_rows, d), F32).at[:b].set(c).at[b].set(c_ctx)
    mod = _mod_call(c_rows, w_mod[lyr], b_mod[lyr]).reshape(mod_rows, n_mod, 1, d)
    mod_l = lambda i: mod[:b, i]
    mod_c = lambda i: mod[b:b + 1, i]

    cast = lambda w: w.astype(BF16)
    w1 = (cast(w1_gate[lyr]), cast(w1_up[lyr]), cast(w1_down[lyr]))
    w2 = (cast(w2_gate[lyr]), cast(w2_up[lyr]), cast(w2_down[lyr]))

    x1 = _ffn_call(x.reshape(b * seq, d), mod_l(0), mod_l(1), mod_l(2), g_pre[lyr, 0], g_post[lyr, 0],
                   *w1, tokens_per_row=seq, res_w=0.5).reshape(b, seq, d)
    ctx1 = _ffn_call(ctx.reshape(b * ctx_len, d), mod_c(0), mod_c(1), mod_c(2), g_pre[lyr, 0],
                     g_post[lyr, 0], *w1, tokens_per_row=None, res_w=0.5).reshape(b, ctx_len, d)

    wi = w_in[lyr]
    n_gate = 4 * HEADS
    wqkv = cast(wi[:, :3 * GDN_W])
    wzf = cast(jnp.concatenate([wi[:, 3 * GDN_W:4 * GDN_W], wi[:, 4 * GDN_W + n_gate:]], axis=1))
    wab = cast(jnp.pad(wi[:, 4 * GDN_W:4 * GDN_W + n_gate], ((0, 0), (0, LANES - n_gate))))
    gatep = jnp.zeros((2, LANES), F32)
    gatep = gatep.at[0, :2 * HEADS].set(a_log[lyr].reshape(-1)).at[1, :2 * HEADS].set(dt_bias[lyr].reshape(-1))
    cs = jnp.asarray(_channel_table(seq), BF16)
    q, k, v, z, fa, fb, gb = _inproj_call(ctx1, x1, mod_c(3), mod_c(4), mod_l(3), mod_l(4),
                                          g_pre[lyr, 1], wqkv, wzf, wab, conv_w[lyr], gatep, cs)

    o_f, o_b = _gdn_call(q, k, v, gb, seq)
    wc, ws_neg = _dft_tables(seq)
    fr = _dft_call(wc, ws_neg, fa, fb)
    x2 = _outproj_call(o_f, o_b, z, fr, x1, mod_l(5), g_onorm[lyr], g_post[lyr, 1], cast(w_out[lyr]))

    out = _ffn_call(x2.reshape(b * seq, d), mod_l(6), mod_l(7), mod_l(8), g_pre[lyr, 2], g_post[lyr, 2],
                    *w2, tokens_per_row=seq, res_w=0.5)
    return out.reshape(b, seq, d)
```

```python
import functools
import math

import jax
import jax.numpy as jnp
import numpy as np
from jax import lax
from jax.experimental import pallas as pl
from jax.experimental.pallas import tpu as pltpu

F32 = jnp.float32
BF16 = jnp.bfloat16

RMS_EPS = 1e-6
L2_EPS = 1e-6
HEADS = 8
HEAD_DIM = 128
GDN_W = HEADS * HEAD_DIM
CONV_K = 5
CHUNK = 64
SUPER = 256
N_LEVELS = 6
LANES = 128
FFN_ROW_CHUNK = 256
VMEM_LIMIT = 56 * 1024 * 1024


def _rms(x):
    return x * lax.rsqrt(jnp.mean(x * x, axis=-1, keepdims=True) + RMS_EPS)


def _silu(x):
    return x * jax.nn.sigmoid(x)


def _dot(a, b):
    return jnp.dot(a, b, preferred_element_type=F32)


def _dot_nt(a, b):
    return lax.dot_general(a, b, (((1,), (1,)), ((), ())), preferred_element_type=F32)


def _dot_tn(a, b):
    return lax.dot_general(a, b, (((0,), (0,)), ((), ())), preferred_element_type=F32)


def _dot_exact(a, b):
    return jnp.dot(a, b, preferred_element_type=F32, precision=lax.Precision.HIGHEST)


def _mod_kernel(c_ref, w_ref, b_ref, o_ref):
    s = _silu(c_ref[...]).astype(BF16)
    o_ref[...] = _dot(s, w_ref[...].astype(BF16)) + b_ref[...]


def _mod_call(c_rows, w_mod, b_mod):
    rows, d = c_rows.shape
    n = w_mod.shape[1]
    tn = 1024
    return pl.pallas_call(
        _mod_kernel,
        grid=(n // tn,),
        in_specs=[
            pl.BlockSpec((rows, d), lambda j: (0, 0)),
            pl.BlockSpec((d, tn), lambda j: (0, j)),
            pl.BlockSpec((1, tn), lambda j: (0, j)),
        ],
        out_specs=pl.BlockSpec((rows, tn), lambda j: (0, j)),
        out_shape=jax.ShapeDtypeStruct((rows, n), F32),
        compiler_params=pltpu.CompilerParams(
            dimension_semantics=("arbitrary",), vmem_limit_bytes=VMEM_LIMIT),
        name="mod",
    )(c_rows, w_mod, b_mod.reshape(1, n))


def _ffn_kernel(x_ref, shift_ref, scale_ref, gate_ref, gpre_ref, gpost_ref,
                wg_ref, wu_ref, wd_ref, o_ref, h_scr, *, res_w):
    j = pl.program_id(1)
    last = pl.num_programs(1) - 1
    tm = x_ref.shape[0]
    row_chunks = [pl.ds(r0, FFN_ROW_CHUNK) for r0 in range(0, tm, FFN_ROW_CHUNK)]

    def partial_out(h):
        g = _dot(h, wg_ref[...])
        u = _dot(h, wu_ref[...])
        return _dot((_silu(g) * u).astype(BF16), wd_ref[...])

    @pl.when(j == 0)
    def _():
        pre_gain = gpre_ref[...] * (1.0 + scale_ref[0])
        for rows in row_chunks:
            h = (_rms(x_ref[rows, :]) * pre_gain + shift_ref[0]).astype(BF16)
            h_scr[rows, :] = h
            o_ref[rows, :] = partial_out(h)

    @pl.when((j > 0) & (j < last))
    def _():
        o_ref[...] += partial_out(h_scr[...])

    @pl.when(j == last)
    def _():
        post_gain = (res_w * gate_ref[0]) * gpost_ref[...]
        for rows in row_chunks:
            y = o_ref[rows, :] + partial_out(h_scr[rows, :])
            o_ref[rows, :] = x_ref[rows, :] + _rms(y) * post_gain


def _ffn_call(x2d, shift, scale, gate, g_pre, g_post, wg, wu, wd, *, tokens_per_row, res_w):
    n_tok, d = x2d.shape
    f = wg.shape[1]
    tm = min(512, n_tok)
    tf = 512
    assert n_tok % tm == 0 and f % tf == 0 and f // tf >= 2 and tm % FFN_ROW_CHUNK == 0
    if tokens_per_row is None:
        row = lambda i, j: (0, 0, 0)
    else:
        assert tokens_per_row % tm == 0
        tiles_per_row = tokens_per_row // tm
        row = lambda i, j: (i // tiles_per_row, 0, 0)
    vec = pl.BlockSpec((1, 1, d), row)
    const = pl.BlockSpec((1, d), lambda i, j: (0, 0))
    return pl.pallas_call(
        functools.partial(_ffn_kernel, res_w=res_w),
        grid=(n_tok // tm, f // tf),
        in_specs=[
            pl.BlockSpec((tm, d), lambda i, j: (i, 0)),
            vec, vec, vec, const, const,
            pl.BlockSpec((d, tf), lambda i, j: (0, j)),
            pl.BlockSpec((d, tf), lambda i, j: (0, j)),
            pl.BlockSpec((tf, d), lambda i, j: (j, 0)),
        ],
        out_specs=pl.BlockSpec((tm, d), lambda i, j: (i, 0)),
        out_shape=jax.ShapeDtypeStruct((n_tok, d), F32),
        scratch_shapes=[pltpu.VMEM((tm, d), BF16)],
        compiler_params=pltpu.CompilerParams(
            dimension_semantics=("arbitrary", "arbitrary"), vmem_limit_bytes=VMEM_LIMIT),
        name="ffn",
    )(x2d, shift, scale, gate, g_pre.reshape(1, d), g_post.reshape(1, d), wg, wu, wd)


def _conv_silu(p, w, pos, row_len):
    n = p.shape[0]
    acc = p * w[CONV_K // 2:CONV_K // 2 + 1, :]
    for tap in range(CONV_K):
        s = tap - CONV_K // 2
        if s == 0:
            continue
        shifted = pltpu.roll(p, (n - s) % n, axis=0)
        ok = (pos + s >= 0) & (pos + s < row_len)
        acc = acc + jnp.where(ok, shifted, 0.0) * w[tap:tap + 1, :]
    return _silu(acc)


def _inproj_kernel(ctx_ref, x_ref, shc_ref, scc_ref, shl_ref, scl_ref, gpre_ref,
                   wqkv_ref, wzf_ref, wab_ref, convw_ref, gatep_ref, cs_ref, perm_ref,
                   q_ref, k_ref, v_ref, z_ref, fa_ref, fb_ref, gb_ref):
    s = pl.program_id(1)

    def body(src_ref, shift, scale, row_len, latent):
        x = src_ref[0]
        h = (_rms(x) * gpre_ref[...] * (1.0 + scale) + shift).astype(BF16)
        pos = lax.broadcasted_iota(jnp.int32, (SUPER, 1), 0) % row_len
        for idx, out_ref in enumerate((q_ref, k_ref, v_ref)):
            cols = slice(idx * GDN_W, (idx + 1) * GDN_W)
            y = _conv_silu(_dot(h, wqkv_ref[:, cols]), convw_ref[:, cols], pos, row_len)
            for hh in range(HEADS):
                yh = y[:, hh * HEAD_DIM:(hh + 1) * HEAD_DIM]
                if idx < 2:
                    yh = yh * lax.rsqrt(jnp.sum(yh * yh, axis=-1, keepdims=True) + L2_EPS)
                if idx == 0:
                    yh = yh * (HEAD_DIM ** -0.5)
                out_ref[0, hh] = yh
        ab = _dot(h, wab_ref[...])
        a_log = gatep_ref[0:1, :]
        dt_bias = gatep_ref[1:2, :]
        xa = ab + dt_bias
        softplus = jnp.maximum(xa, 0.0) + jnp.log1p(jnp.exp(-jnp.abs(xa)))
        g = -jnp.exp(a_log) * softplus
        lane = lax.broadcasted_iota(jnp.int32, (1, LANES), 1)
        gb_ref[0] = jnp.where(lane < 2 * HEADS, g, jax.nn.sigmoid(ab))
        if latent:
            z_ref[0] = _dot(h, wzf_ref[:, :GDN_W])
            f = _dot(h, wzf_ref[:, GDN_W:]).astype(BF16)
            parts = [_dot(f[:, gg * HEAD_DIM:(gg + 1) * HEAD_DIM], cs_ref[...]).astype(BF16)
                     for gg in range(HEADS)]
            half = SUPER // 2
            for out_ref, lo in ((fa_ref, 0), (fb_ref, HEAD_DIM)):
                cat = jnp.concatenate([p[:, lo:lo + HEAD_DIM] for p in parts], axis=1)
                perm = _dot(perm_ref[...], cat).astype(BF16)
                out_ref[0, 0] = perm[:half]
                out_ref[0, 1] = perm[half:]

    @pl.when(s == 0)
    def _():
        body(ctx_ref, shc_ref[0], scc_ref[0], ctx_ref.shape[1], False)

    @pl.when(s > 0)
    def _():
        body(x_ref, shl_ref[0], scl_ref[0], CHUNK, True)


def _inproj_call(ctx1, x1, sh_c, sc_c, sh_l, sc_l, g_pre, wqkv, wzf, wab, conv_w, gatep, cs):
    b, seq, d = x1.shape
    assert ctx1.shape[1] == SUPER and seq % SUPER == 0
    ns = 1 + seq // SUPER
    ltot = SUPER + seq
    half = SUPER // 2
    src = np.concatenate([np.arange(0, SUPER, 2), np.arange(1, SUPER, 2)])
    perm = jnp.asarray(np.eye(SUPER, dtype=np.float32)[src], BF16)
    par_spec = pl.BlockSpec((1, 2, half, GDN_W), lambda bi, s: (bi, 0, jnp.maximum(s - 1, 0), 0))
    lat = lambda bi, s: (bi, jnp.maximum(s - 1, 0), 0)
    resident = lambda shape: pl.BlockSpec(shape, lambda bi, s: (0, 0), pipeline_mode=pl.Buffered(1))
    head_spec = pl.BlockSpec((1, HEADS, SUPER, HEAD_DIM), lambda bi, s: (bi, 0, s, 0))
    lat_spec = pl.BlockSpec((1, SUPER, GDN_W), lat)
    return pl.pallas_call(
        _inproj_kernel,
        grid=(b, ns),
        in_specs=[
            pl.BlockSpec((1, SUPER, d), lambda bi, s: (bi, 0, 0)),
            pl.BlockSpec((1, SUPER, d), lat),
            pl.BlockSpec((1, 1, d), lambda bi, s: (0, 0, 0)),
            pl.BlockSpec((1, 1, d), lambda bi, s: (0, 0, 0)),
            pl.BlockSpec((1, 1, d), lambda bi, s: (bi, 0, 0)),
            pl.BlockSpec((1, 1, d), lambda bi, s: (bi, 0, 0)),
            pl.BlockSpec((1, d), lambda bi, s: (0, 0)),
            resident(wqkv.shape), resident(wzf.shape), resident(wab.shape),
            pl.BlockSpec(conv_w.shape, lambda bi, s: (0, 0)),
            pl.BlockSpec(gatep.shape, lambda bi, s: (0, 0)),
            pl.BlockSpec(cs.shape, lambda bi, s: (0, 0)),
            pl.BlockSpec(perm.shape, lambda bi, s: (0, 0)),
        ],
        out_specs=[head_spec, head_spec, head_spec, lat_spec, par_spec, par_spec,
                   pl.BlockSpec((1, SUPER, LANES), lambda bi, s: (bi, s, 0))],
        out_shape=[
            jax.ShapeDtypeStruct((b, HEADS, ltot, HEAD_DIM), F32),
            jax.ShapeDtypeStruct((b, HEADS, ltot, HEAD_DIM), F32),
            jax.ShapeDtypeStruct((b, HEADS, ltot, HEAD_DIM), F32),
            jax.ShapeDtypeStruct((b, seq, GDN_W), F32),
            jax.ShapeDtypeStruct((b, 2, seq // 2, GDN_W), BF16),
            jax.ShapeDtypeStruct((b, 2, seq // 2, GDN_W), BF16),
            jax.ShapeDtypeStruct((b, ltot, LANES), F32),
        ],
        compiler_params=pltpu.CompilerParams(
            dimension_semantics=("arbitrary", "arbitrary"), vmem_limit_bytes=VMEM_LIMIT),
        name="inproj",
    )(ctx1, x1, sh_c, sc_c, sh_l, sc_l, g_pre.reshape(1, d), wqkv, wzf, wab, conv_w, gatep, cs, perm)


def _gdn_masks():
    i = np.arange(SUPER)[:, None]
    j = np.arange(SUPER)[None, :]
    same = (i // CHUNK) == (j // CHUNK)
    neg = np.zeros((2, SUPER, SUPER), np.float32)
    incl = np.zeros((2, SUPER, SUPER), np.float32)
    lvl_masks = np.zeros((2, N_LEVELS, SUPER, SUPER), np.float32)
    for d in range(2):
        a, c = (i, j) if d == 0 else (j, i)
        causal = same & (a >= c)
        neg[d] = np.where(causal, 0.0, -np.inf)
        incl[d] = causal
        for lvl in range(N_LEVELS):
            lvl_masks[d, lvl] = (((a >> (lvl + 1)) == (c >> (lvl + 1)))
                                 & (((a >> lvl) & 1) == 1) & (((c >> lvl) & 1) == 0))
    return jnp.asarray(neg), jnp.asarray(incl), jnp.asarray(lvl_masks, BF16)


def _gdn_kernel(qf_ref, kf_ref, vf_ref, gbf_ref, qb_ref, kb_ref, vb_ref, gbb_ref,
                neg_ref, incl_ref, lvl_ref, of_ref, ob_ref, h_scr):
    s = pl.program_id(1)

    @pl.when(s == 0)
    def _():
        h_scr[...] = jnp.zeros_like(h_scr)

    n_chunks = SUPER // CHUNK
    rows = lax.broadcasted_iota(jnp.int32, (SUPER, SUPER), 0)
    cols = lax.broadcasted_iota(jnp.int32, (SUPER, SUPER), 1)
    eye = (rows == cols).astype(F32)
    dirs = ((qf_ref, kf_ref, vf_ref, gbf_ref, of_ref), (qb_ref, kb_ref, vb_ref, gbb_ref, ob_ref))
    problems = [(hh, d) for hh in range(HEADS) for d in range(2)]

    gates = []
    for d in range(2):
        gb = dirs[d][3][0]
        incl = incl_ref[d]
        g_cum = _dot_exact(incl, gb)
        g_tot = _dot_exact(incl + incl.T - eye, gb)
        gates.append((gb, g_cum, g_cum.T, g_tot))

    n16, t_inv, pqk = {}, {}, {}
    for hh, d in problems:
        q_ref, k_ref = dirs[d][0], dirs[d][1]
        gb, g_cum, g_cum_t, _ = gates[d]
        col = d * HEADS + hh
        k16 = k_ref[0, hh].astype(BF16)
        beta = gb[:, 2 * HEADS + col:2 * HEADS + col + 1]
        dec = jnp.exp((g_cum[:, col:col + 1] - g_cum_t[col:col + 1, :]) + neg_ref[d])
        n_raw = (beta * dec) * _dot_nt(k16, k16)
        t_inv[hh, d] = eye - n_raw * lvl_ref[d, 0].astype(F32)
        n16[hh, d] = n_raw.astype(BF16)
        pqk[hh, d] = (_dot_nt(q_ref[0, hh].astype(BF16), k16) * dec).astype(BF16)

    for lvl in range(1, N_LEVELS):
        t16 = {p: t_inv[p].astype(BF16) for p in problems}
        inner = {p: _dot(n16[p] * lvl_ref[p[1], lvl], t16[p]).astype(BF16) for p in problems}
        for p in problems:
            t_inv[p] = t_inv[p] - _dot(t16[p], inner[p])

    v_t, w16, q_g, k_d, g_last = {}, {}, {}, {}, {}
    for hh, d in problems:
        q_ref, k_ref, v_ref = dirs[d][0], dirs[d][1], dirs[d][2]
        gb, g_cum, _, g_tot = gates[d]
        col = d * HEADS + hh
        k = k_ref[0, hh]
        beta = gb[:, 2 * HEADS + col:2 * HEADS + col + 1]
        gc = g_cum[:, col:col + 1]
        gt = g_tot[:, col:col + 1]
        gamma = jnp.exp(gc)
        rhs = jnp.concatenate([beta * v_ref[0, hh], (beta * gamma) * k], axis=1).astype(BF16)
        sol = _dot(t_inv[hh, d].astype(BF16), rhs)
        v_t[hh, d] = sol[:, :HEAD_DIM]
        w16[hh, d] = sol[:, HEAD_DIM:].astype(BF16)
        q_g[hh, d] = (q_ref[0, hh] * gamma).astype(BF16)
        k_d[hh, d] = (k * jnp.exp(gt - gc)).astype(BF16)
        g_last[hh, d] = jnp.exp(gt)

    h = {(hh, d): h_scr[d, hh] for hh, d in problems}
    for step in range(n_chunks):
        rs = {p: (step if p[1] == 0 else n_chunks - 1 - step) * CHUNK for p in problems}
        h16 = {p: h[p].astype(BF16) for p in problems}
        u16 = {}
        for p in problems:
            r = slice(rs[p], rs[p] + CHUNK)
            u16[p] = (v_t[p][r] - _dot(w16[p][r], h16[p])).astype(BF16)
        for p in problems:
            hh, d = p
            r = slice(rs[p], rs[p] + CHUNK)
            dirs[d][4][0, hh, r, :] = _dot(q_g[p][r], h16[p]) + _dot(pqk[p][r, r], u16[p])
            h[p] = h[p] * g_last[p][rs[p]:rs[p] + 1, :] + _dot_tn(k_d[p][r], u16[p])
    for hh, d in problems:
        h_scr[d, hh] = h[hh, d]


def _gdn_call(q, k, v, gb, seq):
    b = q.shape[0]
    neg, incl, lvl_masks = _gdn_masks()
    n_lat = seq // SUPER
    ns = 1 + n_lat
    fwd = lambda bi, s: (bi, 0, s, 0)
    bwd = lambda bi, s: (bi, 0, jnp.where(s == 0, 0, ns - s), 0)
    fwd_g = lambda bi, s: (bi, s, 0)
    bwd_g = lambda bi, s: (bi, jnp.where(s == 0, 0, ns - s), 0)
    head = lambda im: pl.BlockSpec((1, HEADS, SUPER, HEAD_DIM), im)
    out_f = lambda bi, s: (bi, 0, jnp.maximum(s - 1, 0), 0)
    out_b = lambda bi, s: (bi, 0, jnp.where(s == 0, n_lat - 1, n_lat - s), 0)
    o_shape = jax.ShapeDtypeStruct((b, HEADS, seq, HEAD_DIM), F32)
    return pl.pallas_call(
        _gdn_kernel,
        grid=(b, ns),
        in_specs=[head(fwd), head(fwd), head(fwd), pl.BlockSpec((1, SUPER, LANES), fwd_g),
                  head(bwd), head(bwd), head(bwd), pl.BlockSpec((1, SUPER, LANES), bwd_g),
                  pl.BlockSpec(neg.shape, lambda bi, s: (0, 0, 0)),
                  pl.BlockSpec(incl.shape, lambda bi, s: (0, 0, 0)),
                  pl.BlockSpec(lvl_masks.shape, lambda bi, s: (0, 0, 0, 0))],
        out_specs=[head(out_f), head(out_b)],
        out_shape=[o_shape, o_shape],
        scratch_shapes=[pltpu.VMEM((2, HEADS, HEAD_DIM, HEAD_DIM), F32)],
        compiler_params=pltpu.CompilerParams(
            dimension_semantics=("arbitrary", "arbitrary"), vmem_limit_bytes=VMEM_LIMIT),
        name="gdn",
    )(q, k, v, gb, q, k, v, gb, neg, incl, lvl_masks)


def _dft_kernel(ce_ref, se_ref, co_ref, so_ref, fa_ref, fb_ref, o_ref, even_scr, odd_scr):
    kk = pl.program_id(2)

    @pl.when(kk == 0)
    def _():
        even_scr[...] = jnp.zeros_like(even_scr)
        odd_scr[...] = jnp.zeros_like(odd_scr)

    even_scr[...] += _dot(ce_ref[...], fa_ref[0, 0]) + _dot(se_ref[...], fb_ref[0, 0])
    odd_scr[...] += _dot(co_ref[...], fa_ref[0, 1]) + _dot(so_ref[...], fb_ref[0, 1])

    @pl.when(kk == pl.num_programs(2) - 1)
    def _():
        o_ref[0, 0] = even_scr[...] + odd_scr[...]
        o_ref[0, 1] = even_scr[...] - odd_scr[...]


def _dft_call(tables, fa, fb):
    b, _, half, n = fa.shape
    tm = min(1024, half)
    tk = min(512, half)
    table = pl.BlockSpec((tm, tk), lambda bi, m, kk: (m, kk))
    data = pl.BlockSpec((1, 2, tk, n), lambda bi, m, kk: (bi, 0, kk, 0))
    out = pl.pallas_call(
        _dft_kernel,
        grid=(b, half // tm, half // tk),
        in_specs=[table, table, table, table, data, data],
        out_specs=pl.BlockSpec((1, 2, tm, n), lambda bi, m, kk: (bi, 0, m, 0)),
        out_shape=jax.ShapeDtypeStruct((b, 2, half, n), F32),
        scratch_shapes=[pltpu.VMEM((tm, n), F32), pltpu.VMEM((tm, n), F32)],
        compiler_params=pltpu.CompilerParams(
            dimension_semantics=("arbitrary", "arbitrary", "arbitrary"), vmem_limit_bytes=VMEM_LIMIT),
        name="dft",
    )(*tables, fa, fb)
    return out.reshape(b, 2 * half, n)


def _outproj_kernel(of_ref, ob_ref, z_ref, fr_ref, x_ref, gate_ref, gon_ref, gpost_ref, w_ref, o_ref):
    z = z_ref[0]
    parts = []
    for hh in range(HEADS):
        o = of_ref[0, hh] + ob_ref[0, hh]
        zh = z[:, hh * HEAD_DIM:(hh + 1) * HEAD_DIM]
        parts.append((_rms(o) * gon_ref[...] * _silu(zh)).astype(BF16))
    parts.append(fr_ref[0].astype(BF16))
    cat = jnp.concatenate(parts, axis=1)
    y = _rms(_dot(cat, w_ref[...])) * gpost_ref[...]
    o_ref[0] = x_ref[0] + gate_ref[0] * y


def _outproj_call(o_f, o_b, z, fr, x1, gate, g_onorm, g_post, w_out):
    b, seq, d = x1.shape
    tm = min(256, seq)
    tok = lambda n: pl.BlockSpec((1, tm, n), lambda bi, i: (bi, i, 0))
    head = pl.BlockSpec((1, HEADS, tm, HEAD_DIM), lambda bi, i: (bi, 0, i, 0))
    return pl.pallas_call(
        _outproj_kernel,
        grid=(b, seq // tm),
        in_specs=[head, head, tok(GDN_W), tok(GDN_W), tok(d),
                  pl.BlockSpec((1, 1, d), lambda bi, i: (bi, 0, 0)),
                  pl.BlockSpec((1, HEAD_DIM), lambda bi, i: (0, 0)),
                  pl.BlockSpec((1, d), lambda bi, i: (0, 0)),
                  pl.BlockSpec(w_out.shape, lambda bi, i: (0, 0), pipeline_mode=pl.Buffered(1))],
        out_specs=tok(d),
        out_shape=jax.ShapeDtypeStruct((b, seq, d), F32),
        compiler_params=pltpu.CompilerParams(
            dimension_semantics=("arbitrary", "arbitrary"), vmem_limit_bytes=VMEM_LIMIT),
        name="outproj",
    )(o_f, o_b, z, fr, x1, gate, g_onorm.reshape(1, HEAD_DIM), g_post.reshape(1, d), w_out)


def _dft_tables(seq):
    half = seq // 2
    p = lax.broadcasted_iota(jnp.int32, (half, half), 0)
    m = lax.broadcasted_iota(jnp.int32, (half, half), 1)
    tables = []
    for parity in range(2):
        ang = ((p * (2 * m + parity)) % seq).astype(F32) * (2.0 * math.pi / seq)
        tables += [jnp.cos(ang).astype(BF16), (-jnp.sin(ang)).astype(BF16)]
    return tables


def _channel_table(seq):
    c = np.arange(HEAD_DIM)
    ang = 2.0 * np.pi * ((c[:, None] * c[None, :]) % HEAD_DIM) / HEAD_DIM
    scale = 1.0 / math.sqrt(seq * HEAD_DIM)
    return np.concatenate([np.cos(ang), np.sin(ang)], axis=1).astype(np.float32) * scale


def kernel(x, c, ctx, c_ctx, w_mod, b_mod, g_pre, g_post, w1_gate, w1_up, w1_down, w_in, conv_w,
           a_log, dt_bias, g_onorm, w_out, w2_gate, w2_up, w2_down):
    b, seq, d = x.shape
    ctx_len = ctx.shape[1]
    depth = w_mod.shape[0]
    assert depth == 1 and ctx_len == SUPER
    n_mod = w_mod.shape[2] // d
    lyr = 0

    mod_rows = 16
    c_rows = jnp.zeros((mod_rows, d), F32).at[:b].set(c).at[b].set(c_ctx)
    mod = _mod_call(c_rows, w_mod[lyr], b_mod[lyr]).reshape(mod_rows, n_mod, 1, d)
    mod_l = lambda i: mod[:b, i]
    mod_c = lambda i: mod[b:b + 1, i]

    cast = lambda w: w.astype(BF16)
    w1 = (cast(w1_gate[lyr]), cast(w1_up[lyr]), cast(w1_down[lyr]))
    w2 = (cast(w2_gate[lyr]), cast(w2_up[lyr]), cast(w2_down[lyr]))

    x1 = _ffn_call(x.reshape(b * seq, d), mod_l(0), mod_l(1), mod_l(2), g_pre[lyr, 0], g_post[lyr, 0],
                   *w1, tokens_per_row=seq, res_w=0.5).reshape(b, seq, d)
    ctx1 = _ffn_call(ctx.reshape(b * ctx_len, d), mod_c(0), mod_c(1), mod_c(2), g_pre[lyr, 0],
                     g_post[lyr, 0], *w1, tokens_per_row=None, res_w=0.5).reshape(b, ctx_len, d)

    wi = w_in[lyr]
    n_gate = 4 * HEADS
    wqkv = cast(wi[:, :3 * GDN_W])
    wzf = cast(jnp.concatenate([wi[:, 3 * GDN_W:4 * GDN_W], wi[:, 4 * GDN_W + n_gate:]], axis=1))
    wab = cast(jnp.pad(wi[:, 4 * GDN_W:4 * GDN_W + n_gate], ((0, 0), (0, LANES - n_gate))))
    gatep = jnp.zeros((2, LANES), F32)
    gatep = gatep.at[0, :2 * HEADS].set(a_log[lyr].reshape(-1)).at[1, :2 * HEADS].set(dt_bias[lyr].reshape(-1))
    cs = jnp.asarray(_channel_table(seq), BF16)
    q, k, v, z, fa, fb, gb = _inproj_call(ctx1, x1, mod_c(3), mod_c(4), mod_l(3), mod_l(4),
                                          g_pre[lyr, 1], wqkv, wzf, wab, conv_w[lyr], gatep, cs)

    o_f, o_b = _gdn_call(q, k, v, gb, seq)
    fr = _dft_call(_dft_tables(seq), fa, fb)
    x2 = _outproj_call(o_f, o_b, z, fr, x1, mod_l(5), g_onorm[lyr], g_post[lyr, 1], cast(w_out[lyr]))

    out = _ffn_call(x2.reshape(b * seq, d), mod_l(6), mod_l(7), mod_l(8), g_pre[lyr, 2], g_post[lyr, 2],
                    *w2, tokens_per_row=seq, res_w=0.5)
    return out.reshape(b, seq, d)
```
